```python
import math
import jax, jax.numpy as jnp
from jax import lax
import numpy as np

D_MODEL = 2048
BATCH = 4
SEQ = 2048
DEPTH = 2

N_MIXERS = 2
N_LAYERS_A = (DEPTH + 1) // 2
N_LAYERS_B = DEPTH // 2
NORM_EPS = 1e-6
PLE_DIM = 256
D_FF = -(-8 * D_MODEL // (3 * 256)) * 256
LRU_WIDTH = D_MODEL
LRU_HEADS = 8
LRU_BLOCK = LRU_WIDTH // LRU_HEADS
CONV_WIDTH = 4
LRU_C = 8.0
SSD_EXPAND = 2
SSD_INNER = SSD_EXPAND * D_MODEL
SSD_HEAD_DIM = 64
SSD_HEADS = SSD_INNER // SSD_HEAD_DIM
SSD_GROUPS = 8
SSD_HEADS_PER_GROUP = SSD_HEADS // SSD_GROUPS
SSD_STATE = 128
SSD_CHUNK = 128
SSD_CONV_DIM = SSD_INNER + 2 * SSD_GROUPS * SSD_STATE
SSD_IN_DIM = SSD_INNER + SSD_CONV_DIM + SSD_HEADS
SSD_NORM_GROUP = SSD_INNER // SSD_GROUPS

kernel_name = "hybrid_rglru_ssd_swiglu_ple"


def rmsnorm(x, g):
    xf = x.astype(jnp.float32)
    y = xf * lax.rsqrt(jnp.mean(xf * xf, axis=-1, keepdims=True) + NORM_EPS)
    return (y * g.astype(jnp.float32)).astype(x.dtype)


def causal_dwconv(x, w, b):
    W = w.shape[0]
    S = x.shape[1]
    xp = jnp.pad(x, ((0, 0), (W - 1, 0), (0, 0)))
    y = b + xp[:, 0:S] * w[0]
    for k in range(1, W):
        y = y + xp[:, k:k + S] * w[k]
    return y


def _lin_combine(c1, c2):
    a1, b1 = c1
    a2, b2 = c2
    return (a1 * a2, a2 * b1 + b2)


def rglru_mixer(u, w_in, conv_w, conv_b, w_gate_r, b_gate_r, w_gate_i, b_gate_i, lam, w_out):
    bsz, S, _ = u.shape
    xy = u @ w_in
    xr, yg = jnp.split(xy, 2, axis=-1)
    yg = jax.nn.gelu(yg, approximate=True)
    xr = causal_dwconv(xr, conv_w, conv_b)
    xb = xr.reshape(bsz, S, LRU_HEADS, LRU_BLOCK)
    r = jax.nn.sigmoid(jnp.einsum('bshi,hij->bshj', xb, w_gate_r) + b_gate_r).reshape(bsz, S, LRU_WIDTH)
    i = jax.nn.sigmoid(jnp.einsum('bshi,hij->bshj', xb, w_gate_i) + b_gate_i).reshape(bsz, S, LRU_WIDTH)
    log_a = -LRU_C * r.astype(jnp.float32) * jax.nn.softplus(-lam.astype(jnp.float32))
    a = jnp.exp(log_a)
    mult = jnp.sqrt(-jnp.expm1(2.0 * log_a))
    bterm = mult * (i * xr).astype(jnp.float32)
    _, hs = lax.associative_scan(_lin_combine, (a, bterm), axis=1)
    y = hs.astype(u.dtype) * yg
    return y @ w_out


def segsum(x):
    T = x.shape[-1]
    cs = jnp.cumsum(x, axis=-1)
    diff = cs[..., :, None] - cs[..., None, :]
    mask = jnp.tril(jnp.ones((T, T), dtype=bool))
    return jnp.where(mask, diff, -jnp.inf)


def ssd_scan(x, dt, A, Bm, Cm):
    b, S, H, P = x.shape
    L = SSD_CHUNK
    nc = S // L
    G, E, N = SSD_GROUPS, SSD_HEADS_PER_GROUP, SSD_STATE
    X = (x * dt[..., None]).reshape(b, nc, L, G, E, P)
    Adt = (A * dt).reshape(b, nc, L, G, E).transpose(0, 3, 4, 1, 2)
    Bc = Bm.reshape(b, nc, L, G, N)
    Cc = Cm.reshape(b, nc, L, G, N)
    A_cs = jnp.cumsum(Adt, axis=-1)
    Lmat = jnp.exp(segsum(Adt))
    CB = jnp.einsum('bclgn,bcsgn->bgcls', Cc, Bc)
    scores = CB[:, :, None] * Lmat
    y_diag = jnp.einsum('bgecls,bcsgep->bclgep', scores, X)
    decay_states = jnp.exp(A_cs[..., -1:] - A_cs)
    states = jnp.einsum('bclgn,bgecl,bclgep->bcgepn', Bc, decay_states, X)
    states = jnp.concatenate([jnp.zeros_like(states[:, :1]), states], axis=1)
    chunk_tot = jnp.pad(A_cs[..., -1], ((0, 0), (0, 0), (0, 0), (1, 0)))
    decay_chunk = jnp.exp(segsum(chunk_tot))
    states_in = jnp.einsum('bgezc,bcgepn->bzgepn', decay_chunk, states)[:, :-1]
    y_off = jnp.einsum('bclgn,bcgepn,bgecl->bclgep', Cc, states_in, jnp.exp(A_cs))
    return (y_diag + y_off).reshape(b, S, H, P)


def ssd_mixer(u, w_in, conv_w, conv_b, dt_bias, a_log, d_skip, norm_g, w_out):
    b, S, _ = u.shape
    G, N = SSD_GROUPS, SSD_STATE
    zxbcdt = u @ w_in
    z = zxbcdt[..., :SSD_INNER]
    xbc = zxbcdt[..., SSD_INNER:SSD_INNER + SSD_CONV_DIM]
    dt = zxbcdt[..., SSD_INNER + SSD_CONV_DIM:]
    xbc = jax.nn.silu(causal_dwconv(xbc, conv_w, conv_b))
    xs = xbc[..., :SSD_INNER].reshape(b, S, SSD_HEADS, SSD_HEAD_DIM).astype(jnp.float32)
    Bm = xbc[..., SSD_INNER:SSD_INNER + G * N].reshape(b, S, G, N).astype(jnp.float32)
    Cm = xbc[..., SSD_INNER + G * N:].reshape(b, S, G, N).astype(jnp.float32)
    dt = jax.nn.softplus(dt.astype(jnp.float32) + dt_bias.astype(jnp.float32))
    A = -jnp.exp(a_log.astype(jnp.float32))
    y = ssd_scan(xs, dt, A, Bm, Cm)
    y = y + d_skip.astype(jnp.float32)[:, None] * xs
    y = y.reshape(b, S, SSD_INNER) * jax.nn.silu(z.astype(jnp.float32))
    yg = y.reshape(b, S, SSD_GROUPS, SSD_NORM_GROUP)
    yg = yg * lax.rsqrt(jnp.mean(yg * yg, axis=-1, keepdims=True) + NORM_EPS)
    y = (yg.reshape(b, S, SSD_INNER) * norm_g.astype(jnp.float32)).astype(u.dtype)
    return y @ w_out


def swiglu(u, w_gate, w_up, w_down):
    return (jax.nn.silu(u @ w_gate) * (u @ w_up)) @ w_down


def setup_inputs(seed: int = 0) -> dict:
    key = jax.random.key(seed)
    ks = jax.random.split(key, 32)
    f32 = jnp.float32
    nrm = lambda k, shape, scale: jax.random.normal(k, shape, f32) * scale
    gain = lambda k, shape: 1.0 + 0.02 * jax.random.normal(k, shape, f32)
    x = jax.random.normal(ks[0], (BATCH, SEQ, D_MODEL), f32)
    p = jax.random.normal(ks[1], (DEPTH, BATCH, SEQ, PLE_DIM), f32)
    norm_mix_g = gain(ks[2], (DEPTH, D_MODEL))
    norm_ffn_g = gain(ks[3], (DEPTH, D_MODEL))
    norm_ple_g = gain(ks[4], (DEPTH, D_MODEL))
    final_norm_g = gain(ks[5], (D_MODEL,))
    a_w_in = nrm(ks[6], (N_LAYERS_A, D_MODEL, 2 * LRU_WIDTH), D_MODEL ** -0.5)
    a_conv_w = nrm(ks[7], (N_LAYERS_A, CONV_WIDTH, LRU_WIDTH), CONV_WIDTH ** -0.5)
    a_conv_b = nrm(ks[8], (N_LAYERS_A, LRU_WIDTH), 0.01)
    a_w_gate_r = nrm(ks[9], (N_LAYERS_A, LRU_HEADS, LRU_BLOCK, LRU_BLOCK), LRU_BLOCK ** -0.5)
    a_b_gate_r = nrm(ks[10], (N_LAYERS_A, LRU_HEADS, LRU_BLOCK), 0.01)
    a_w_gate_i = nrm(ks[11], (N_LAYERS_A, LRU_HEADS, LRU_BLOCK, LRU_BLOCK), LRU_BLOCK ** -0.5)
    a_b_gate_i = nrm(ks[12], (N_LAYERS_A, LRU_HEADS, LRU_BLOCK), 0.01)
    u_a = jax.random.uniform(ks[13], (N_LAYERS_A, LRU_WIDTH), f32, 0.9, 0.999)
    s_a = u_a ** (1.0 / LRU_C)
    a_lambda = jnp.log(s_a) - jnp.log1p(-s_a)
    a_w_out = nrm(ks[14], (N_LAYERS_A, LRU_WIDTH, D_MODEL), LRU_WIDTH ** -0.5)
    b_w_in = nrm(ks[15], (N_LAYERS_B, D_MODEL, SSD_IN_DIM), D_MODEL ** -0.5)
    b_conv_w = nrm(ks[16], (N_LAYERS_B, CONV_WIDTH, SSD_CONV_DIM), CONV_WIDTH ** -0.5)
    b_conv_b = nrm(ks[17], (N_LAYERS_B, SSD_CONV_DIM), 0.01)
    dt0 = jnp.exp(jax.random.uniform(ks[18], (N_LAYERS_B, SSD_HEADS), f32, math.log(1e-3), math.log(1e-1)))
    b_dt_bias = dt0 + jnp.log(-jnp.expm1(-dt0))
    b_a_log = jnp.log(jax.random.uniform(ks[19], (N_LAYERS_B, SSD_HEADS), f32, 1.0, 16.0))
    b_d_skip = 1.0 + 0.1 * jax.random.normal(ks[20], (N_LAYERS_B, SSD_HEADS), f32)
    b_norm_g = gain(ks[21], (N_LAYERS_B, SSD_INNER))
    b_w_out = nrm(ks[22], (N_LAYERS_B, SSD_INNER, D_MODEL), SSD_INNER ** -0.5)
    ffn_w_gate = nrm(ks[23], (DEPTH, D_MODEL, D_FF), D_MODEL ** -0.5)
    ffn_w_up = nrm(ks[24], (DEPTH, D_MODEL, D_FF), D_MODEL ** -0.5)
    ffn_w_down = nrm(ks[25], (DEPTH, D_FF, D_MODEL), D_FF ** -0.5)
    ple_w_proj = nrm(ks[26], (DEPTH, PLE_DIM, D_MODEL), PLE_DIM ** -0.5)
    ple_w_gate = nrm(ks[27], (DEPTH, D_MODEL, D_MODEL), D_MODEL ** -0.5)
    return {"x": x, "p": p, "norm_mix_g": norm_mix_g, "norm_ffn_g": norm_ffn_g,
            "norm_ple_g": norm_ple_g, "final_norm_g": final_norm_g,
            "a_w_in": a_w_in, "a_conv_w": a_conv_w, "a_conv_b": a_conv_b,
            "a_w_gate_r": a_w_gate_r, "a_b_gate_r": a_b_gate_r,
            "a_w_gate_i": a_w_gate_i, "a_b_gate_i": a_b_gate_i,
            "a_lambda": a_lambda, "a_w_out": a_w_out,
            "b_w_in": b_w_in, "b_conv_w": b_conv_w, "b_conv_b": b_conv_b,
            "b_dt_bias": b_dt_bias, "b_a_log": b_a_log, "b_d_skip": b_d_skip,
            "b_norm_g": b_norm_g, "b_w_out": b_w_out,
            "ffn_w_gate": ffn_w_gate, "ffn_w_up": ffn_w_up, "ffn_w_down": ffn_w_down,
            "ple_w_proj": ple_w_proj, "ple_w_gate": ple_w_gate}


def reference(x, p, norm_mix_g, norm_ffn_g, norm_ple_g, final_norm_g,
              a_w_in, a_conv_w, a_conv_b, a_w_gate_r, a_b_gate_r, a_w_gate_i, a_b_gate_i,
              a_lambda, a_w_out,
              b_w_in, b_conv_w, b_conv_b, b_dt_bias, b_a_log, b_d_skip, b_norm_g, b_w_out,
              ffn_w_gate, ffn_w_up, ffn_w_down, ple_w_proj, ple_w_gate):
    h = x
    for i in range(DEPTH):
        u = rmsnorm(h, norm_mix_g[i])
        j = i // N_MIXERS
        if i % N_MIXERS == 0:
            m = rglru_mixer(u, a_w_in[j], a_conv_w[j], a_conv_b[j], a_w_gate_r[j], a_b_gate_r[j],
                            a_w_gate_i[j], a_b_gate_i[j], a_lambda[j], a_w_out[j])
        else:
            m = ssd_mixer(u, b_w_in[j], b_conv_w[j], b_conv_b[j], b_dt_bias[j], b_a_log[j],
                          b_d_skip[j], b_norm_g[j], b_w_out[j])
        h = h + m
        h = h + swiglu(rmsnorm(h, norm_ffn_g[i]), ffn_w_gate[i], ffn_w_up[i], ffn_w_down[i])
        gate = jax.nn.sigmoid(rmsnorm(h, norm_ple_g[i]) @ ple_w_gate[i])
        h = h + gate * (p[i].astype(h.dtype) @ ple_w_proj[i])
    return rmsnorm(h, final_norm_g)
```

```python
import functools

import jax
import jax.numpy as jnp
from jax import lax
from jax.experimental import pallas as pl
from jax.experimental.pallas import tpu as pltpu

F32 = jnp.float32
BF16 = jnp.bfloat16

NORM_EPS = 1e-6
LRU_C = 8.0
SSD_GROUPS = 8
SSD_STATE = 128
SSD_CHUNK = 128

V7X_LANES = 128
V7X_SUBLANES = 8
V7X_VMEM_BYTES = 64 * 1024 * 1024
VMEM_LIMIT_BYTES = V7X_VMEM_BYTES - 8 * 1024 * 1024

TM = 1024
TN = 1024
TN_HALF = 512
TM_NORM = 512
LRU_ROWS = 256


def _params(*semantics):
    return pltpu.CompilerParams(dimension_semantics=semantics, vmem_limit_bytes=VMEM_LIMIT_BYTES)


def _dot(a, b):
    return jnp.dot(a, b, preferred_element_type=F32)


def _rmsnorm_kernel(h_ref, g_ref, o_ref):
    x = h_ref[...]
    ms = jnp.mean(x * x, axis=-1, keepdims=True)
    o_ref[...] = ((x * lax.rsqrt(ms + NORM_EPS)) * g_ref[...]).astype(o_ref.dtype)


def _rmsnorm(h, g, out_dtype):
    m, d = h.shape
    return pl.pallas_call(
        _rmsnorm_kernel,
        grid=(m // TM_NORM,),
        in_specs=[pl.BlockSpec((TM_NORM, d), lambda i: (i, 0)), pl.BlockSpec((1, d), lambda i: (0, 0))],
        out_specs=pl.BlockSpec((TM_NORM, d), lambda i: (i, 0)),
        out_shape=jax.ShapeDtypeStruct((m, d), out_dtype),
        compiler_params=_params("parallel"),
        name="rmsnorm",
    )(h, g.reshape(1, d))


def _blocked_matmul(body, out_dtype, n_out, tn, row_ops, col_ops, tile_ops, name):
    m = row_ops[0].shape[0]
    in_specs = [pl.BlockSpec((TM, a.shape[1]), lambda i, j: (i, 0)) for a in row_ops]
    in_specs += [pl.BlockSpec((a.shape[0], tn), lambda i, j, off=off: (0, j + off)) for a, off in col_ops]
    in_specs += [pl.BlockSpec((TM, tn), lambda i, j: (i, j)) for _ in tile_ops]
    return pl.pallas_call(
        body,
        grid=(m // TM, n_out // tn),
        in_specs=in_specs,
        out_specs=pl.BlockSpec((TM, tn), lambda i, j: (i, j)),
        out_shape=jax.ShapeDtypeStruct((m, n_out), out_dtype),
        compiler_params=_params("parallel", "arbitrary"),
        name=name,
    )(*row_ops, *[a for a, _ in col_ops], *tile_ops)


def _mm_kernel(x_ref, w_ref, o_ref, *, act):
    acc = _dot(x_ref[...], w_ref[...])
    o_ref[...] = (acc if act is None else act(acc)).astype(o_ref.dtype)


def _mm_residual_kernel(x_ref, w_ref, h_ref, o_ref):
    o_ref[...] = h_ref[...] + _dot(x_ref[...], w_ref[...])


def _swiglu_kernel(x_ref, wg_ref, wu_ref, o_ref):
    x = x_ref[...]
    o_ref[...] = (jax.nn.silu(_dot(x, wg_ref[...])) * _dot(x, wu_ref[...])).astype(o_ref.dtype)


def _ple_kernel(u_ref, p_ref, wg_ref, wp_ref, h_ref, o_ref):
    gate = jax.nn.sigmoid(_dot(u_ref[...], wg_ref[...]))
    proj = _dot(p_ref[...].astype(BF16), wp_ref[...])
    o_ref[...] = h_ref[...] + gate * proj


def _causal_conv(x, pad_ref, cw_ref, cb_ref):
    rows = x.shape[0]
    pad_ref[V7X_SUBLANES:V7X_SUBLANES + rows, :] = x
    y = cb_ref[...] + pad_ref[5:5 + rows, :] * cw_ref[0:1, :]
    y = y + pad_ref[6:6 + rows, :] * cw_ref[1:2, :]
    y = y + pad_ref[7:7 + rows, :] * cw_ref[2:3, :]
    y = y + x * cw_ref[3:4, :]
    pad_ref[0:V7X_SUBLANES, :] = x[rows - V7X_SUBLANES:rows, :]
    return y


def _linear_scan(a, b, h0):
    rows, cols = a.shape
    tiles = rows // V7X_SUBLANES
    a3 = a.reshape(tiles, V7X_SUBLANES, cols)
    b3 = b.reshape(tiles, V7X_SUBLANES, cols)
    sub = lax.broadcasted_iota(jnp.int32, a3.shape, 1)
    for k in (1, 2, 4):
        keep = sub >= k
        b3 = jnp.where(keep, b3 + a3 * pltpu.roll(b3, k, 1), b3)
        a3 = jnp.where(keep, a3 * pltpu.roll(a3, k, 1), a3)
    h = h0
    out = []
    for t in range(tiles):
        ht = b3[t] + a3[t] * h
        out.append(ht)
        h = ht[V7X_SUBLANES - 1:V7X_SUBLANES, :]
    return jnp.concatenate(out, axis=0), h


def _lru_kernel(xr_ref, yg_ref, cw_ref, cb_ref, wr_ref, br_ref, wi_ref, bi_ref, lam_ref, o_ref,
                pad_ref, h_ref, *, heads):
    @pl.when(pl.program_id(1) == 0)
    def _():
        pad_ref[0:V7X_SUBLANES, :] = jnp.zeros((V7X_SUBLANES, pad_ref.shape[1]), F32)
        h_ref[...] = jnp.zeros_like(h_ref)

    width = xr_ref.shape[1]
    hw = width // heads
    conv = _causal_conv(xr_ref[...], pad_ref, cw_ref, cb_ref)
    softplus_neg_lam = jax.nn.softplus(-lam_ref[...])
    for hd in range(heads):
        sl = slice(hd * hw, (hd + 1) * hw)
        xh = conv[:, sl]
        xb = xh.astype(BF16)
        r = jax.nn.sigmoid(_dot(xb, wr_ref[hd]) + br_ref[:, sl])
        ig = jax.nn.sigmoid(_dot(xb, wi_ref[hd]) + bi_ref[:, sl])
        log_a = (-LRU_C * r) * softplus_neg_lam[:, sl]
        a = jnp.exp(log_a)
        mult = jnp.sqrt(-jnp.tanh(log_a) * (a * a + 1.0))
        hs, h_last = _linear_scan(a, mult * (ig * xh), h_ref[0:1, sl])
        h_ref[0:1, sl] = h_last
        o_ref[:, sl] = (hs * yg_ref[:, sl]).astype(o_ref.dtype)


def _lru_block(xr, yg, conv_w, conv_b, w_r, b_r, w_i, b_i, lam, batch):
    t, width = xr.shape
    heads = w_r.shape[0]
    chunks = t // batch // LRU_ROWS
    row_spec = pl.BlockSpec((LRU_ROWS, width), lambda b, c: (b * chunks + c, 0))
    vec_spec = pl.BlockSpec((1, width), lambda b, c: (0, 0))
    gate_spec = pl.BlockSpec(w_r.shape, lambda b, c: (0, 0, 0))
    return pl.pallas_call(
        functools.partial(_lru_kernel, heads=heads),
        grid=(batch, chunks),
        in_specs=[row_spec, row_spec, pl.BlockSpec(conv_w.shape, lambda b, c: (0, 0)), vec_spec,
                  gate_spec, vec_spec, gate_spec, vec_spec, vec_spec],
        out_specs=row_spec,
        out_shape=jax.ShapeDtypeStruct((t, width), BF16),
        scratch_shapes=[pltpu.VMEM((LRU_ROWS + V7X_SUBLANES, width), F32),
                        pltpu.VMEM((V7X_SUBLANES, width), F32)],
        compiler_params=_params("parallel", "arbitrary"),
        name="rglru",
    )(xr, yg, conv_w, conv_b.reshape(1, width), w_r, b_r.reshape(1, width), w_i, b_i.reshape(1, width),
      lam.reshape(1, width))


def _ssd_dt_kernel(raw_ref, bias_ref, alog_ref, dt_t_ref, acs_t_ref, *, heads):
    rows, lanes = raw_ref.shape
    live = lax.broadcasted_iota(jnp.int32, (rows, lanes), 1) < heads
    dt = jnp.where(live, jax.nn.softplus(raw_ref[...] + bias_ref[...]), 0.0)
    acs = (-jnp.exp(alog_ref[...])) * dt
    row = lax.broadcasted_iota(jnp.int32, (rows, lanes), 0)
    k = 1
    while k < rows:
        acs = jnp.where(row >= k, acs + pltpu.roll(acs, k, 0), acs)
        k *= 2
    dt_t_ref[0] = dt.T[:heads, :]
    acs_t_ref[0] = acs.T[:heads, :]


def _ssd_dt(raw, bias, a_log, heads):
    t, lanes = raw.shape
    chunks = t // SSD_CHUNK
    out = jax.ShapeDtypeStruct((chunks, heads, SSD_CHUNK), F32)
    out_spec = pl.BlockSpec((1, heads, SSD_CHUNK), lambda i: (i, 0, 0))
    return pl.pallas_call(
        functools.partial(_ssd_dt_kernel, heads=heads),
        grid=(chunks,),
        in_specs=[pl.BlockSpec((SSD_CHUNK, lanes), lambda i: (i, 0)),
                  pl.BlockSpec((1, lanes), lambda i: (0, 0)), pl.BlockSpec((1, lanes), lambda i: (0, 0))],
        out_specs=[out_spec, out_spec],
        out_shape=[out, out],
        compiler_params=_params("parallel"),
        name="ssd_dt",
    )(raw, bias, a_log)


def _ssd_kernel(z_ref, x_ref, b_ref, c_ref, cwx_ref, cwb_ref, cwc_ref, cbx_ref, cbb_ref, cbc_ref,
                dt_t_ref, acs_t_ref, dskip_ref, ng_ref, o_ref, xpad_ref, bpad_ref, cpad_ref, state_ref,
                *, head_dim):
    @pl.when(pl.program_id(2) == 0)
    def _():
        for ref in (xpad_ref, bpad_ref, cpad_ref):
            ref[0:V7X_SUBLANES, :] = jnp.zeros((V7X_SUBLANES, ref.shape[1]), F32)
        state_ref[...] = jnp.zeros_like(state_ref)

    rows, width = x_ref.shape
    heads = width // head_dim
    xc = jax.nn.silu(_causal_conv(x_ref[...], xpad_ref, cwx_ref, cbx_ref))
    bc = jax.nn.silu(_causal_conv(b_ref[...], bpad_ref, cwb_ref, cbb_ref))
    cc = jax.nn.silu(_causal_conv(c_ref[...], cpad_ref, cwc_ref, cbc_ref))
    bcb = bc.astype(BF16)
    ccb = cc.astype(BF16)
    cb = lax.dot_general(ccb, bcb, (((1,), (1,)), ((), ())), preferred_element_type=F32)

    dt_t = dt_t_ref[0]
    acs_t = acs_t_ref[0]
    fill = jnp.zeros((rows - heads, rows), F32)
    dt_c = jnp.concatenate([dt_t, fill], axis=0).T
    acs_c = jnp.concatenate([acs_t, fill], axis=0).T

    causal = (lax.broadcasted_iota(jnp.int32, (rows, rows), 0)
              >= lax.broadcasted_iota(jnp.int32, (rows, rows), 1))
    pair_w = 2 * head_dim
    first = lax.broadcasted_iota(jnp.int32, (rows, pair_w), 1) < head_dim

    y_diag = []
    acs_x = []
    dt_x = []
    for pr in range(heads // 2):
        xp = xc[:, pr * pair_w:(pr + 1) * pair_w]
        acc = None
        acs_cols = []
        dt_cols = []
        for q in range(2):
            e = 2 * pr + q
            acs_col = jnp.broadcast_to(acs_c[:, e:e + 1], (rows, rows))
            acs_row = jnp.broadcast_to(acs_t[e:e + 1, :], (rows, rows))
            dt_row = jnp.broadcast_to(dt_t[e:e + 1, :], (rows, rows))
            decay = jnp.exp(jnp.where(causal, acs_col - acs_row, -jnp.inf))
            scores = ((cb * decay) * dt_row).astype(BF16)
            rhs = jnp.where(first if q == 0 else jnp.logical_not(first), xp, 0.0).astype(BF16)
            part = _dot(scores, rhs)
            acc = part if acc is None else acc + part
            acs_cols.append(acs_col[:, :pair_w])
            dt_cols.append(jnp.broadcast_to(dt_c[:, e:e + 1], (rows, pair_w)))
        y_diag.append(acc)
        acs_x.append(jnp.where(first, acs_cols[0], acs_cols[1]))
        dt_x.append(jnp.where(first, dt_cols[0], dt_cols[1]))
    y_diag = jnp.concatenate(y_diag, axis=1)
    acs_x = jnp.concatenate(acs_x, axis=1)
    dt_x = jnp.concatenate(dt_x, axis=1)
    acs_last = acs_x[rows - 1:rows, :]

    state = state_ref[...]
    y_off = _dot(ccb, state.astype(BF16)) * jnp.exp(acs_x)
    xw = (xc * (dt_x * jnp.exp(acs_last - acs_x))).astype(BF16)
    update = lax.dot_general(bcb, xw, (((0,), (0,)), ((), ())), preferred_element_type=F32)
    state_ref[...] = state * jnp.exp(acs_last) + update

    y = (y_diag + y_off) + dskip_ref[...] * xc
    y = y * jax.nn.silu(z_ref[...])
    ms = jnp.mean(y * y, axis=-1, keepdims=True)
    o_ref[...] = ((y * lax.rsqrt(ms + NORM_EPS)) * ng_ref[...]).astype(o_ref.dtype)


def _ssd_block(zxbc, dt_t, acs_t, conv_w, conv_b, d_skip_x, norm_g, batch, inner, head_dim):
    t = zxbc.shape[0]
    chunks = t // batch // SSD_CHUNK
    gw = inner // SSD_GROUPS
    hpg = gw // head_dim
    x0 = inner // gw
    b0 = 2 * inner // SSD_STATE
    c0 = b0 + SSD_GROUPS
    cb0 = inner // SSD_STATE
    cc0 = cb0 + SSD_GROUPS

    def rows(width, first):
        return pl.BlockSpec((SSD_CHUNK, width), lambda b, g, c: (b * chunks + c, first + g))

    def cols(height, width, first):
        return pl.BlockSpec((height, width), lambda b, g, c: (0, first + g))

    head_spec = pl.BlockSpec((1, hpg, SSD_CHUNK), lambda b, g, c: (b * chunks + c, g, 0))
    kw = conv_w.shape[0]
    return pl.pallas_call(
        functools.partial(_ssd_kernel, head_dim=head_dim),
        grid=(batch, SSD_GROUPS, chunks),
        in_specs=[rows(gw, 0), rows(gw, x0), rows(SSD_STATE, b0), rows(SSD_STATE, c0),
                  cols(kw, gw, 0), cols(kw, SSD_STATE, cb0), cols(kw, SSD_STATE, cc0),
                  cols(1, gw, 0), cols(1, SSD_STATE, cb0), cols(1, SSD_STATE, cc0),
                  head_spec, head_spec, cols(1, gw, 0), cols(1, gw, 0)],
        out_specs=rows(gw, 0),
        out_shape=jax.ShapeDtypeStruct((t, inner), BF16),
        scratch_shapes=[pltpu.VMEM((SSD_CHUNK + V7X_SUBLANES, gw), F32),
                        pltpu.VMEM((SSD_CHUNK + V7X_SUBLANES, SSD_STATE), F32),
                        pltpu.VMEM((SSD_CHUNK + V7X_SUBLANES, SSD_STATE), F32),
                        pltpu.VMEM((SSD_STATE, gw), F32)],
        compiler_params=_params("parallel", "parallel", "arbitrary"),
        name="ssd",
    )(zxbc, zxbc, zxbc, zxbc, conv_w, conv_w, conv_w, conv_b, conv_b, conv_b, dt_t, acs_t, d_skip_x, norm_g)


def _channel_mix(h, norm_g, w_gate, w_up, w_down):
    u = _rmsnorm(h, norm_g, BF16)
    act = _blocked_matmul(_swiglu_kernel, BF16, w_gate.shape[1], TN_HALF, [u],
                          [(w_gate.astype(BF16), 0), (w_up.astype(BF16), 0)], [], "swiglu_up")
    return _blocked_matmul(_mm_residual_kernel, F32, h.shape[1], TN_HALF, [act],
                           [(w_down.astype(BF16), 0)], [h], "swiglu_down")


def _embed_mix(h, norm_g, p, w_gate, w_proj):
    u = _rmsnorm(h, norm_g, BF16)
    return _blocked_matmul(_ple_kernel, F32, h.shape[1], TN, [u, p],
                           [(w_gate.astype(BF16), 0), (w_proj.astype(BF16), 0)], [h], "ple")


def _rglru_mix(h, norm_g, w_in, conv_w, conv_b, w_r, b_r, w_i, b_i, lam, w_out, batch):
    width = w_out.shape[0]
    u = _rmsnorm(h, norm_g, BF16)
    w_in = w_in.astype(BF16)
    xr = _blocked_matmul(functools.partial(_mm_kernel, act=None), F32, width, TN, [u], [(w_in, 0)], [],
                         "rglru_in_x")
    gelu = functools.partial(jax.nn.gelu, approximate=True)
    yg = _blocked_matmul(functools.partial(_mm_kernel, act=gelu), F32, width, TN, [u],
                         [(w_in, width // TN)], [], "rglru_in_gate")
    y = _lru_block(xr, yg, conv_w, conv_b, w_r.astype(BF16), b_r, w_i.astype(BF16), b_i, lam, batch)
    return _blocked_matmul(_mm_residual_kernel, F32, h.shape[1], TN, [y], [(w_out.astype(BF16), 0)], [h],
                           "rglru_out")


def _ssd_mix(h, norm_g, w_in, conv_w, conv_b, dt_bias, a_log, d_skip, ssd_norm_g, w_out, batch):
    inner = w_out.shape[0]
    heads = dt_bias.shape[0]
    head_dim = inner // heads
    n_zxbc = inner + conv_w.shape[1]
    u = _rmsnorm(h, norm_g, BF16)
    zxbc = _blocked_matmul(functools.partial(_mm_kernel, act=None), F32, n_zxbc, TN, [u],
                           [(w_in.astype(BF16), 0)], [], "ssd_in")
    lane_pad = V7X_LANES - heads
    w_dt = jnp.pad(w_in[:, n_zxbc:], ((0, 0), (0, lane_pad))).astype(BF16)
    dt_raw = _blocked_matmul(functools.partial(_mm_kernel, act=None), F32, V7X_LANES, V7X_LANES, [u],
                             [(w_dt, 0)], [], "ssd_in_dt")
    dt_t, acs_t = _ssd_dt(dt_raw, jnp.pad(dt_bias, (0, lane_pad)).reshape(1, V7X_LANES),
                          jnp.pad(a_log, (0, lane_pad)).reshape(1, V7X_LANES), heads)
    d_skip_x = jnp.repeat(d_skip, head_dim).reshape(1, inner)
    y = _ssd_block(zxbc, dt_t, acs_t, conv_w, conv_b.reshape(1, -1), d_skip_x, ssd_norm_g.reshape(1, inner),
                   batch, inner, head_dim)
    return _blocked_matmul(_mm_residual_kernel, F32, h.shape[1], TN_HALF, [y], [(w_out.astype(BF16), 0)], [h],
                           "ssd_out")


def kernel(x, p, norm_mix_g, norm_ffn_g, norm_ple_g, final_norm_g, a_w_in, a_conv_w, a_conv_b, a_w_gate_r, a_b_gate_r, a_w_gate_i, a_b_gate_i, a_lambda, a_w_out, b_w_in, b_conv_w, b_conv_b, b_dt_bias, b_a_log, b_d_skip, b_norm_g, b_w_out, ffn_w_gate, ffn_w_up, ffn_w_down, ple_w_proj, ple_w_gate):
    batch, seq, d_model = x.shape
    depth = p.shape[0]
    h = x.reshape(batch * seq, d_model)
    for i in range(depth):
        j = i // 2
        if i % 2 == 0:
            h = _rglru_mix(h, norm_mix_g[i], a_w_in[j], a_conv_w[j], a_conv_b[j], a_w_gate_r[j],
                           a_b_gate_r[j].reshape(-1), a_w_gate_i[j], a_b_gate_i[j].reshape(-1), a_lambda[j],
                           a_w_out[j], batch)
        else:
            h = _ssd_mix(h, norm_mix_g[i], b_w_in[j], b_conv_w[j], b_conv_b[j], b_dt_bias[j], b_a_log[j],
                         b_d_skip[j], b_norm_g[j], b_w_out[j], batch)
        h = _channel_mix(h, norm_ffn_g[i], ffn_w_gate[i], ffn_w_up[i], ffn_w_down[i])
        h = _embed_mix(h, norm_ple_g[i], p[i].reshape(batch * seq, -1), ple_w_gate[i], ple_w_proj[i])
    return _rmsnorm(h, final_norm_g, F32).reshape(batch, seq, d_model)
```

```python
import functools
import math

import jax
import jax.numpy as jnp
from jax import lax
from jax.experimental import pallas as pl
from jax.experimental.pallas import tpu as pltpu

F32 = jnp.float32
BF16 = jnp.bfloat16
LOG2_E = math.log2(math.e)

NORM_EPS = 1e-6
LRU_C = 8.0
SSD_GROUPS = 8
SSD_STATE = 128
SSD_CHUNK = 128

V7X_LANES = 128
V7X_SUBLANES = 8
V7X_VMEM_BYTES = 64 * 1024 * 1024
VMEM_LIMIT_BYTES = V7X_VMEM_BYTES - 8 * 1024 * 1024

TM = 1024
TN = 1024
TN_HALF = 512
TN_QUARTER = 256
TM_NORM = 512
LRU_ROWS = 256
SSD_ROWS = 512


def _params(*semantics):
    return pltpu.CompilerParams(dimension_semantics=semantics, vmem_limit_bytes=VMEM_LIMIT_BYTES)


def _dot(a, b):
    return jnp.dot(a, b, preferred_element_type=F32)


def _wdot(x_ref, w_ref):
    return _dot(x_ref[...], w_ref[...].astype(BF16))


def _rmsnorm_kernel(h_ref, g_ref, o_ref):
    x = h_ref[...]
    ms = jnp.mean(x * x, axis=-1, keepdims=True)
    o_ref[...] = ((x * lax.rsqrt(ms + NORM_EPS)) * g_ref[...]).astype(o_ref.dtype)


def _rmsnorm(h, g, out_dtype):
    m, d = h.shape
    return pl.pallas_call(
        _rmsnorm_kernel,
        grid=(m // TM_NORM,),
        in_specs=[pl.BlockSpec((TM_NORM, d), lambda i: (i, 0)), pl.BlockSpec((1, d), lambda i: (0, 0))],
        out_specs=pl.BlockSpec((TM_NORM, d), lambda i: (i, 0)),
        out_shape=jax.ShapeDtypeStruct((m, d), out_dtype),
        compiler_params=_params("parallel"),
        name="rmsnorm",
    )(h, g.reshape(1, d))


def _blocked_matmul(body, out_dtype, n_out, tn, row_ops, col_ops, tile_ops, name, scratch_shapes=()):
    m = row_ops[0][0].shape[1]
    in_specs = [pl.BlockSpec((None, TM, a.shape[2]), lambda j, i, l=l: (l, i, 0)) for a, l in row_ops]
    in_specs += [pl.BlockSpec((None, a.shape[1], tn), lambda j, i, l=l, off=off: (l, 0, j + off))
                 for a, l, off in col_ops]
    in_specs += [pl.BlockSpec((TM, tn), lambda j, i: (i, j)) for _ in tile_ops]
    return pl.pallas_call(
        body,
        grid=(n_out // tn, m // TM),
        in_specs=in_specs,
        out_specs=pl.BlockSpec((TM, tn), lambda j, i: (i, j)),
        out_shape=jax.ShapeDtypeStruct((m, n_out), out_dtype),
        scratch_shapes=list(scratch_shapes),
        compiler_params=_params("parallel", "arbitrary"),
        name=name,
    )(*[a for a, _ in row_ops], *[a for a, _, _ in col_ops], *tile_ops)


def _mm_act_kernel(x_ref, w_ref, o_ref, *, act):
    o_ref[...] = act(_wdot(x_ref, w_ref)).astype(o_ref.dtype)


def _mm_conv_kernel(x_ref, w_ref, cw_ref, cb_ref, o_ref, halo_ref, *, tiles_per_seq, act):
    @pl.when(pl.program_id(1) % tiles_per_seq == 0)
    def _():
        halo_ref[...] = jnp.zeros_like(halo_ref)

    acc = _wdot(x_ref, w_ref)
    rows = acc.shape[0]
    ext = jnp.concatenate([halo_ref[...], acc], axis=0)
    halo_ref[...] = acc[rows - V7X_SUBLANES:rows, :]
    y = cb_ref[...] + pltpu.roll(ext, 3, 0)[V7X_SUBLANES:, :] * cw_ref[0:1, :]
    y = y + pltpu.roll(ext, 2, 0)[V7X_SUBLANES:, :] * cw_ref[1:2, :]
    y = y + pltpu.roll(ext, 1, 0)[V7X_SUBLANES:, :] * cw_ref[2:3, :]
    y = y + acc * cw_ref[3:4, :]
    o_ref[...] = act(y).astype(o_ref.dtype)


def _mm_residual_kernel(x_ref, w_ref, h_ref, o_ref):
    o_ref[...] = h_ref[...] + _wdot(x_ref, w_ref)


def _swiglu_kernel(x_ref, wg_ref, wu_ref, o_ref):
    o_ref[...] = (jax.nn.silu(_wdot(x_ref, wg_ref)) * _wdot(x_ref, wu_ref)).astype(o_ref.dtype)


def _ple_kernel(u_ref, p_ref, wg_ref, wp_ref, h_ref, o_ref):
    gate = jax.nn.sigmoid(_wdot(u_ref, wg_ref))
    proj = _dot(p_ref[...].astype(BF16), wp_ref[...].astype(BF16))
    o_ref[...] = h_ref[...] + gate * proj


def _identity(v):
    return v


def _conv_matmul(u, w, w_first, conv_w, conv_b, layer, n_out, seq, act, name):
    halo = pltpu.VMEM((V7X_SUBLANES, TN_HALF), F32)
    body = functools.partial(_mm_conv_kernel, tiles_per_seq=seq // TM, act=act)
    return _blocked_matmul(body, F32, n_out, TN_HALF, [(u[None], 0)],
                           [(w, layer, w_first), (conv_w, layer, 0), (conv_b, layer, 0)], [], name,
                           scratch_shapes=[halo])


def _linear_scan(a, b, h0):
    rows, cols = a.shape
    tiles = rows // V7X_SUBLANES
    a3 = a.reshape(tiles, V7X_SUBLANES, cols)
    b3 = b.reshape(tiles, V7X_SUBLANES, cols)
    sub = lax.broadcasted_iota(jnp.int32, a3.shape, 1)
    for k in (1, 2, 4):
        keep = sub >= k
        b3 = jnp.where(keep, b3 + a3 * pltpu.roll(b3, k, 1), b3)
        a3 = jnp.where(keep, a3 * pltpu.roll(a3, k, 1), a3)
    h = h0
    out = []
    for t in range(tiles):
        ht = b3[t] + a3[t] * h
        out.append(ht)
        h = ht[V7X_SUBLANES - 1:V7X_SUBLANES, :]
    return jnp.concatenate(out, axis=0), h


def _lru_kernel(xc_ref, yg_ref, wr_ref, br_ref, wi_ref, bi_ref, lam_ref, o_ref, h_ref, *, heads):
    @pl.when(pl.program_id(1) == 0)
    def _():
        h_ref[...] = jnp.zeros_like(h_ref)

    hw = xc_ref.shape[1] // heads
    softplus_neg_lam = jax.nn.softplus(-lam_ref[...])
    for hd in range(heads):
        sl = slice(hd * hw, (hd + 1) * hw)
        xh = xc_ref[:, sl]
        xb = xh.astype(BF16)
        r = jax.nn.sigmoid(_dot(xb, wr_ref[hd]) + br_ref[:, sl])
        ig = jax.nn.sigmoid(_dot(xb, wi_ref[hd]) + bi_ref[:, sl])
        log_a = (-LRU_C * r) * softplus_neg_lam[:, sl]
        a = jnp.exp(log_a)
        mult = jnp.sqrt(-jnp.tanh(log_a) * (a * a + 1.0))
        hs, h_last = _linear_scan(a, mult * (ig * xh), h_ref[0:1, sl])
        h_ref[0:1, sl] = h_last
        o_ref[:, sl] = (hs * yg_ref[:, sl]).astype(o_ref.dtype)


def _lru_block(xc, yg, w_r, b_r, w_i, b_i, lam, batch):
    t, width = xc.shape
    heads = w_r.shape[0]
    chunks = t // batch // LRU_ROWS
    row_spec = pl.BlockSpec((LRU_ROWS, width), lambda b, c: (b * chunks + c, 0))
    vec_spec = pl.BlockSpec((1, width), lambda b, c: (0, 0))
    gate_spec = pl.BlockSpec(w_r.shape, lambda b, c: (0, 0, 0))
    return pl.pallas_call(
        functools.partial(_lru_kernel, heads=heads),
        grid=(batch, chunks),
        in_specs=[row_spec, row_spec, gate_spec, vec_spec, gate_spec, vec_spec, vec_spec],
        out_specs=row_spec,
        out_shape=jax.ShapeDtypeStruct((t, width), BF16),
        scratch_shapes=[pltpu.VMEM((V7X_SUBLANES, width), F32)],
        compiler_params=_params("parallel", "arbitrary"),
        name="rglru",
    )(xc, yg, w_r, b_r.reshape(1, width), w_i, b_i.reshape(1, width), lam.reshape(1, width))


def _ssd_dt_kernel(u_ref, w_ref, bias_ref, alog_ref, acs_t_ref, r_t_ref, *, heads):
    raw = _wdot(u_ref, w_ref)
    a_neg = -jnp.exp(alog_ref[...])
    row = lax.broadcasted_iota(jnp.int32, (SSD_CHUNK, raw.shape[1]), 0)
    for c in range(raw.shape[0] // SSD_CHUNK):
        dt = jax.nn.softplus(raw[c * SSD_CHUNK:(c + 1) * SSD_CHUNK, :] + bias_ref[...])
        acs = a_neg * dt
        k = 1
        while k < SSD_CHUNK:
            acs = jnp.where(row >= k, acs + pltpu.roll(acs, k, 0), acs)
            k *= 2
        acs = acs * LOG2_E
        acs_t_ref[c] = acs.T[:heads, :]
        r_t_ref[c] = (jnp.log2(dt) - acs).T[:heads, :]


def _ssd_dt(u, w_dt, bias, a_log, heads):
    t, d = u.shape
    lanes = w_dt.shape[1]
    per_step = TM // SSD_CHUNK
    out = jax.ShapeDtypeStruct((t // SSD_CHUNK, heads, SSD_CHUNK), F32)
    out_spec = pl.BlockSpec((per_step, heads, SSD_CHUNK), lambda i: (i, 0, 0))
    vec_spec = pl.BlockSpec((1, lanes), lambda i: (0, 0))
    return pl.pallas_call(
        functools.partial(_ssd_dt_kernel, heads=heads),
        grid=(t // TM,),
        in_specs=[pl.BlockSpec((TM, d), lambda i: (i, 0)), pl.BlockSpec((d, lanes), lambda i: (0, 0)),
                  vec_spec, vec_spec],
        out_specs=[out_spec, out_spec],
        out_shape=[out, out],
        compiler_params=_params("parallel"),
        name="ssd_dt",
    )(u, w_dt, bias, a_log)


def _ssd_chunk(xc, bcb, ccb, acs_t, r_t, state, dskip, zs, ng, head_dim):
    rows, width = xc.shape
    heads = width // head_dim
    cb = lax.dot_general(ccb, bcb, (((1,), (1,)), ((), ())), preferred_element_type=F32)

    fill = jnp.zeros((rows - heads, rows), F32)
    acs_c = jnp.concatenate([acs_t, fill], axis=0).T
    r_c = jnp.concatenate([r_t, fill], axis=0).T

    causal = (lax.broadcasted_iota(jnp.int32, (rows, rows), 0)
              >= lax.broadcasted_iota(jnp.int32, (rows, rows), 1))
    pair_w = 2 * head_dim
    first = lax.broadcasted_iota(jnp.int32, (rows, pair_w), 1) < head_dim

    y_diag = []
    acs_x = []
    r_x = []
    for pr in range(heads // 2):
        xp = xc[:, pr * pair_w:(pr + 1) * pair_w]
        scores = []
        acs_cols = []
        r_cols = []
        for e in (2 * pr, 2 * pr + 1):
            acs_col = jnp.broadcast_to(acs_c[:, e:e + 1], (rows, rows))
            r_row = jnp.broadcast_to(r_t[e:e + 1, :], (rows, rows))
            scores.append((cb * jnp.exp2(jnp.where(causal, acs_col + r_row, -jnp.inf))).astype(BF16))
            acs_cols.append(acs_col[:, :pair_w])
            r_cols.append(jnp.broadcast_to(r_c[:, e:e + 1], (rows, pair_w)))
        lhs = jnp.concatenate(scores, axis=1)
        rhs = jnp.concatenate([jnp.where(first, xp, 0.0), jnp.where(first, 0.0, xp)], axis=0).astype(BF16)
        y_diag.append(_dot(lhs, rhs))
        acs_x.append(jnp.where(first, acs_cols[0], acs_cols[1]))
        r_x.append(jnp.where(first, r_cols[0], r_cols[1]))
    y_diag = jnp.concatenate(y_diag, axis=1)
    acs_x = jnp.concatenate(acs_x, axis=1)
    r_x = jnp.concatenate(r_x, axis=1)
    acs_last = acs_x[rows - 1:rows, :]

    y_off = _dot(ccb, state.astype(BF16)) * jnp.exp2(acs_x)
    xw = (xc * jnp.exp2(acs_last + r_x)).astype(BF16)
    update = lax.dot_general(bcb, xw, (((0,), (0,)), ((), ())), preferred_element_type=F32)
    next_state = state * jnp.exp2(acs_last) + update

    y = (y_diag + y_off) + dskip * xc
    y = y * zs
    ms = jnp.mean(y * y, axis=-1, keepdims=True)
    return ((y * lax.rsqrt(ms + NORM_EPS)) * ng).astype(BF16), next_state


def _ssd_kernel(zs_ref, x_ref, b_ref, c_ref, acs_t_ref, r_t_ref, dskip_ref, ng_ref, o_ref, state_ref,
                *, head_dim):
    @pl.when(pl.program_id(2) == 0)
    def _():
        state_ref[...] = jnp.zeros_like(state_ref)

    state = state_ref[...]
    for c in range(x_ref.shape[0] // SSD_CHUNK):
        sl = slice(c * SSD_CHUNK, (c + 1) * SSD_CHUNK)
        o_ref[sl, :], state = _ssd_chunk(
            x_ref[sl, :], b_ref[sl, :].astype(BF16), c_ref[sl, :].astype(BF16), acs_t_ref[c], r_t_ref[c], state,
            dskip_ref[...], zs_ref[sl, :], ng_ref[...], head_dim)
    state_ref[...] = state


def _ssd_block(zs, xbc, acs_t, r_t, d_skip_x, norm_g, batch, head_dim):
    t, inner = zs.shape
    steps = t // batch // SSD_ROWS
    per_step = SSD_ROWS // SSD_CHUNK
    gw = inner // SSD_GROUPS
    hpg = gw // head_dim
    b0 = inner // SSD_STATE
    c0 = b0 + SSD_GROUPS

    def rows(width, first):
        return pl.BlockSpec((SSD_ROWS, width), lambda b, g, s: (b * steps + s, first + g))

    head_spec = pl.BlockSpec((per_step, hpg, SSD_CHUNK), lambda b, g, s: (b * steps + s, g, 0))
    vec_spec = pl.BlockSpec((1, gw), lambda b, g, s: (0, g))
    return pl.pallas_call(
        functools.partial(_ssd_kernel, head_dim=head_dim),
        grid=(batch, SSD_GROUPS, steps),
        in_specs=[rows(gw, 0), rows(gw, 0), rows(SSD_STATE, b0), rows(SSD_STATE, c0),
                  head_spec, head_spec, vec_spec, vec_spec],
        out_specs=rows(gw, 0),
        out_shape=jax.ShapeDtypeStruct((t, inner), BF16),
        scratch_shapes=[pltpu.VMEM((SSD_STATE, gw), F32)],
        compiler_params=_params("parallel", "parallel", "arbitrary"),
        name="ssd",
    )(zs, xbc, xbc, xbc, acs_t, r_t, d_skip_x, norm_g)


def _act(u):
    return (u[None], 0)


def _channel_mix(h, norm_g, w_gate, w_up, w_down, layer):
    u = _rmsnorm(h, norm_g, BF16)
    act = _blocked_matmul(_swiglu_kernel, BF16, w_gate.shape[2], TN_HALF, [_act(u)],
                          [(w_gate, layer, 0), (w_up, layer, 0)], [], "swiglu_up")
    return _blocked_matmul(_mm_residual_kernel, F32, h.shape[1], TN_QUARTER, [_act(act)], [(w_down, layer, 0)],
                           [h], "swiglu_down")


def _embed_mix(h, norm_g, p, w_gate, w_proj, layer):
    u = _rmsnorm(h, norm_g, BF16)
    return _blocked_matmul(_ple_kernel, F32, h.shape[1], TN, [_act(u), (p, layer)],
                           [(w_gate, layer, 0), (w_proj, layer, 0)], [h], "ple")


def _rglru_mix(h, norm_g, w_in, conv_w, conv_b, w_r, b_r, w_i, b_i, lam, w_out, layer, batch):
    width = w_out.shape[1]
    seq = h.shape[0] // batch
    u = _rmsnorm(h, norm_g, BF16)
    xc = _conv_matmul(u, w_in, 0, conv_w, conv_b.reshape(-1, 1, width), layer, width, seq, _identity,
                      "rglru_in_x")
    gelu = functools.partial(jax.nn.gelu, approximate=True)
    yg = _blocked_matmul(functools.partial(_mm_act_kernel, act=gelu), F32, width, TN, [_act(u)],
                         [(w_in, layer, width // TN)], [], "rglru_in_gate")
    y = _lru_block(xc, yg, w_r[layer].astype(BF16), b_r[layer].reshape(-1), w_i[layer].astype(BF16),
                   b_i[layer].reshape(-1), lam[layer], batch)
    return _blocked_matmul(_mm_residual_kernel, F32, h.shape[1], TN, [_act(y)], [(w_out, layer, 0)], [h],
                           "rglru_out")


def _ssd_mix(h, norm_g, w_in, conv_w, conv_b, dt_bias, a_log, d_skip, ssd_norm_g, w_out, layer, batch):
    inner = w_out.shape[1]
    heads = dt_bias.shape[1]
    head_dim = inner // heads
    conv_dim = conv_w.shape[2]
    seq = h.shape[0] // batch
    u = _rmsnorm(h, norm_g, BF16)
    zs = _blocked_matmul(functools.partial(_mm_act_kernel, act=jax.nn.silu), F32, inner, TN, [_act(u)],
                         [(w_in, layer, 0)], [], "ssd_in_z")
    xbc = _conv_matmul(u, w_in, inner // TN_HALF, conv_w, conv_b.reshape(-1, 1, conv_dim), layer, conv_dim, seq,
                       jax.nn.silu, "ssd_in_xbc")
    lane_pad = V7X_LANES - heads
    w_dt = jnp.pad(w_in[layer, :, inner + conv_dim:], ((0, 0), (0, lane_pad)))
    acs_t, r_t = _ssd_dt(u, w_dt, jnp.pad(dt_bias[layer], (0, lane_pad)).reshape(1, V7X_LANES),
                         jnp.pad(a_log[layer], (0, lane_pad)).reshape(1, V7X_LANES), heads)
    d_skip_x = jnp.repeat(d_skip[layer], head_dim).reshape(1, inner)
    y = _ssd_block(zs, xbc, acs_t, r_t, d_skip_x, ssd_norm_g[layer].reshape(1, inner), batch, head_dim)
    return _blocked_matmul(_mm_residual_kernel, F32, h.shape[1], TN_HALF, [_act(y)], [(w_out, layer, 0)], [h],
                           "ssd_out")


def kernel(x, p, norm_mix_g, norm_ffn_g, norm_ple_g, final_norm_g, a_w_in, a_conv_w, a_conv_b, a_w_gate_r, a_b_gate_r, a_w_gate_i, a_b_gate_i, a_lambda, a_w_out, b_w_in, b_conv_w, b_conv_b, b_dt_bias, b_a_log, b_d_skip, b_norm_g, b_w_out, ffn_w_gate, ffn_w_up, ffn_w_down, ple_w_proj, ple_w_gate):
    batch, seq, d_model = x.shape
    depth = p.shape[0]
    h = x.reshape(batch * seq, d_model)
    p = p.reshape(depth, batch * seq, -1)
    for i in range(depth):
        j = i // 2
        if i % 2 == 0:
            h = _rglru_mix(h, norm_mix_g[i], a_w_in, a_conv_w, a_conv_b, a_w_gate_r, a_b_gate_r, a_w_gate_i,
                           a_b_gate_i, a_lambda, a_w_out, j, batch)
        else:
            h = _ssd_mix(h, norm_mix_g[i], b_w_in, b_conv_w, b_conv_b, b_dt_bias, b_a_log, b_d_skip, b_norm_g,
                         b_w_out, j, batch)
        h = _channel_mix(h, norm_ffn_g[i], ffn_w_gate, ffn_w_up, ffn_w_down, i)
        h = _embed_mix(h, norm_ple_g[i], p, ple_w_gate, ple_w_proj, i)
    return _rmsnorm(h, final_norm_g, F32).reshape(batch, seq, d_model)
```
